```python
import math
import numpy as np
import jax
import jax.numpy as jnp
from jax import lax

D_MODEL = 2048
BATCH = 8
SEQ = 4096
DEPTH = 4

DN_HEADS = 4
DN_DK = 128
DN_DV = 128
DN_CONV = 4
DN_CHUNK = 64
DN_QKV = DN_HEADS * (2 * DN_DK + DN_DV)
AT_HEADS = 4
AT_DH = 128
IX_HEADS = 8
IX_DIM = 64
DSA_TOPK = 256
Q_BLOCK = 128
T5_BUCKETS = 32
T5_MAX_DIST = 128
MB_HEADS = 16
MB_HEADDIM = 64
MB_GROUPS = 2
MB_STATE = 128
MB_CONV = 4
MB_CHUNK = 128
MB_INNER = MB_HEADS * MB_HEADDIM
MB_XBC = MB_INNER + 2 * MB_GROUPS * MB_STATE
D_MIX = DN_HEADS * DN_DV + AT_HEADS * AT_DH + MB_INNER
IN_SPLITS = (DN_QKV, DN_HEADS * DN_DV, DN_HEADS, DN_HEADS,
             AT_HEADS * AT_DH, AT_HEADS * AT_DH, AT_HEADS * AT_DH,
             IX_HEADS * IX_DIM, IX_DIM, IX_HEADS,
             MB_INNER, MB_XBC, MB_HEADS)
D_IN = sum(IN_SPLITS)
N_EXPERTS = 64
TOP_K = 8
MOE_D_FF = 384
SHARED_D_FF = 384
ROUTED_SCALE = 2.5
MOE_BLOCK = 256
DEEPNORM_ALPHA = (2 * DEPTH) ** 0.25
DEEPNORM_BETA = (8 * DEPTH) ** -0.25
LN_EPS = 1e-5
RMS_EPS = 1e-6

kernel_name = 'hybrid_deltanet_dsa_ssd_moe_deepnorm'


def layer_norm(x, g, b):
    xf = x.astype(jnp.float32)
    xc = xf - jnp.mean(xf, -1, keepdims=True)
    var = jnp.mean(xc * xc, -1, keepdims=True)
    y = xc * lax.rsqrt(var + LN_EPS) * g.astype(jnp.float32) + b.astype(jnp.float32)
    return y.astype(x.dtype)


def rms_normalize(x):
    return x * lax.rsqrt(jnp.mean(x * x, -1, keepdims=True) + RMS_EPS)


def l2_normalize(x):
    return x * lax.rsqrt(jnp.sum(x * x, -1, keepdims=True) + RMS_EPS)


def causal_depthwise_conv(x, w):
    k, c = w.shape
    return lax.conv_general_dilated(x, w[:, None, :], (1,), [(k - 1, 0)],
                                    dimension_numbers=('NWC', 'WIO', 'NWC'),
                                    feature_group_count=c)


def decay_matrix(cs):
    c = cs.shape[-1]
    causal = jnp.tril(jnp.ones((c, c), dtype=bool))
    return jnp.exp(jnp.where(causal, cs[..., :, None] - cs[..., None, :], -jnp.inf))


def gated_delta_rule(q, k, v, g, beta):
    b, s, h, dk = q.shape
    dv = v.shape[-1]
    c = DN_CHUNK
    n = s // c

    def chunks(t):
        t = t.reshape((b, n, c, h) + t.shape[3:])
        return jnp.moveaxis(t, (1, 3), (0, 2))

    q = chunks(q) * dk ** -0.5
    k = chunks(k)
    v = chunks(v)
    bt = chunks(beta)[..., None]
    gc = jnp.cumsum(chunks(g), -1)
    decay = decay_matrix(gc)
    strict = jnp.tril(jnp.ones((c, c), dtype=bool), -1)
    kb = k * bt
    a_mat = jnp.where(strict, jnp.einsum('nbhid,nbhjd->nbhij', kb, k) * decay, 0.0)
    eye = jnp.eye(c, dtype=a_mat.dtype)
    rhs = jnp.concatenate([v * bt, kb * jnp.exp(gc)[..., None]], -1)
    sol = lax.linalg.triangular_solve(a_mat + eye, rhs, left_side=True, lower=True,
                                      unit_diagonal=True)
    u, w = sol[..., :dv], sol[..., dv:]
    qk = jnp.einsum('nbhid,nbhjd->nbhij', q, k) * decay
    q_in = q * jnp.exp(gc)[..., None]
    k_out = k * jnp.exp(gc[..., -1:] - gc)[..., None]
    g_tot = jnp.exp(gc[..., -1])

    def step(state, inp):
        u_n, w_n, q_n, qk_n, k_n, g_n = inp
        v_new = u_n - jnp.einsum('bhck,bhkv->bhcv', w_n, state)
        o = jnp.einsum('bhck,bhkv->bhcv', q_n, state) + jnp.einsum('bhij,bhjv->bhiv', qk_n, v_new)
        state = state * g_n[..., None, None] + jnp.einsum('bhck,bhcv->bhkv', k_n, v_new)
        return state, o

    state0 = jnp.zeros((b, h, dk, dv), q.dtype)
    _, o = lax.scan(step, state0, (u, w, q_in, qk, k_out, g_tot))
    return jnp.moveaxis(o, (0, 2), (1, 3)).reshape(b, s, h, dv)


def t5_bucket(dist):
    exact = T5_BUCKETS // 2
    d = jnp.maximum(dist, exact).astype(jnp.float32)
    big = exact + (jnp.log(d / exact) / math.log(T5_MAX_DIST / exact)
                   * (T5_BUCKETS - exact)).astype(jnp.int32)
    return jnp.where(dist < exact, dist, jnp.minimum(big, T5_BUCKETS - 1))


def dsa_attention(q, k, v, qi, ki, wi, rel_bias):
    b, s, h, dh = q.shape
    n_sel = min(DSA_TOPK, s // 4)
    n_blk = s // Q_BLOCK
    kpos = jnp.arange(s)
    gather = jax.vmap(lambda t, idx: t[idx])

    def block(i):
        t0 = i * Q_BLOCK
        qb = lax.dynamic_slice_in_dim(q, t0, Q_BLOCK, 1)
        qib = lax.dynamic_slice_in_dim(qi, t0, Q_BLOCK, 1)
        wib = lax.dynamic_slice_in_dim(wi, t0, Q_BLOCK, 1)
        tpos = t0 + jnp.arange(Q_BLOCK)
        idx_logits = jax.nn.relu(jnp.einsum('bthd,bsd->bths', qib, ki))
        score = jnp.einsum('bth,bths->bts', wib, idx_logits)
        score = jnp.where(kpos[None, None, :] <= tpos[None, :, None], score, -jnp.inf)
        _, sel = lax.top_k(score, n_sel)
        ks = gather(k, sel)
        vs = gather(v, sel)
        dist = tpos[None, :, None] - sel
        bias = rel_bias[t5_bucket(jnp.maximum(dist, 0))]
        logits = jnp.einsum('bthd,btkhd->bthk', qb, ks) * dh ** -0.5 + jnp.swapaxes(bias, -1, -2)
        logits = jnp.where((dist >= 0)[:, :, None, :], logits, -jnp.inf)
        p = jax.nn.softmax(logits, -1)
        return jnp.einsum('bthk,btkhd->bthd', p, vs)

    o = lax.map(block, jnp.arange(n_blk))
    return jnp.moveaxis(o, 0, 1).reshape(b, s, h, dh)


def ssd_scan(x, dt, a_head, bm, cm):
    b, s, g, r, p = x.shape
    c = MB_CHUNK
    n = s // c
    xd = (x * dt[..., None]).reshape(b, n, c, g, r, p)
    la = jnp.moveaxis((dt * a_head).reshape(b, n, c, g, r), 2, -1)
    bm = bm.reshape(b, n, c, g, -1)
    cm = cm.reshape(b, n, c, g, -1)
    cs = jnp.cumsum(la, -1)
    lmat = decay_matrix(cs)
    cb = jnp.einsum('bnlgd,bnsgd->bngls', cm, bm)
    y_diag = jnp.einsum('bngls,bngrls,bnsgrp->bnlgrp', cb, lmat, xd)
    to_end = jnp.exp(cs[..., -1:] - cs)
    st = jnp.einsum('bnsgd,bngrs,bnsgrp->bngrpd', bm, to_end, xd)
    chunk_decay = jnp.exp(cs[..., -1])

    def step(hs, inp):
        st_n, dec_n = inp
        return hs * dec_n[..., None, None] + st_n, hs

    h0 = jnp.zeros((b, g, r, p, bm.shape[-1]), x.dtype)
    _, h_prev = lax.scan(step, h0, (jnp.moveaxis(st, 1, 0), jnp.moveaxis(chunk_decay, 1, 0)))
    h_prev = jnp.moveaxis(h_prev, 0, 1)
    y_off = jnp.einsum('bnlgd,bngrpd,bngrl->bnlgrp', cm, h_prev, jnp.exp(cs))
    return (y_diag + y_off).reshape(b, s, g, r, p)


def mixer_sublayer(x, w_in, dn_conv_w, dn_a_log, dn_dt_bias, dn_norm_w, rel_bias,
                   mb_conv_w, mb_conv_b, mb_a_log, mb_dt_bias, mb_d, mb_norm_w, w_out):
    f32 = jnp.float32
    b, s, _ = x.shape
    proj = jnp.einsum('bsd,de->bse', x, w_in).astype(f32)
    cuts = np.cumsum(IN_SPLITS)[:-1].tolist()
    (dn_qkv, dn_z, dn_b, dn_a, at_q, at_k, at_v, ix_q, ix_k, ix_w,
     mb_z, mb_xbc, mb_dt) = jnp.split(proj, cuts, axis=-1)

    qkv = jax.nn.silu(causal_depthwise_conv(dn_qkv, dn_conv_w.astype(f32)))
    dq, dk, dv = jnp.split(qkv, [DN_HEADS * DN_DK, 2 * DN_HEADS * DN_DK], -1)
    dq = l2_normalize(dq.reshape(b, s, DN_HEADS, DN_DK))
    dk = l2_normalize(dk.reshape(b, s, DN_HEADS, DN_DK))
    dv = dv.reshape(b, s, DN_HEADS, DN_DV)
    dbeta = jax.nn.sigmoid(dn_b)
    dg = -jnp.exp(dn_a_log.astype(f32)) * jax.nn.softplus(dn_a + dn_dt_bias.astype(f32))
    o_dn = gated_delta_rule(dq, dk, dv, dg, dbeta)
    o_dn = rms_normalize(o_dn) * dn_norm_w.astype(f32) * jax.nn.silu(dn_z.reshape(b, s, DN_HEADS, DN_DV))

    o_at = dsa_attention(at_q.reshape(b, s, AT_HEADS, AT_DH), at_k.reshape(b, s, AT_HEADS, AT_DH),
                         at_v.reshape(b, s, AT_HEADS, AT_DH), ix_q.reshape(b, s, IX_HEADS, IX_DIM),
                         ix_k, ix_w, rel_bias.astype(f32))

    xbc = jax.nn.silu(causal_depthwise_conv(mb_xbc, mb_conv_w.astype(f32)) + mb_conv_b.astype(f32))
    mx, mB, mC = jnp.split(xbc, [MB_INNER, MB_INNER + MB_GROUPS * MB_STATE], -1)
    rep = MB_HEADS // MB_GROUPS
    mx = mx.reshape(b, s, MB_GROUPS, rep, MB_HEADDIM)
    mB = mB.reshape(b, s, MB_GROUPS, MB_STATE)
    mC = mC.reshape(b, s, MB_GROUPS, MB_STATE)
    mdt = jax.nn.softplus(mb_dt + mb_dt_bias.astype(f32)).reshape(b, s, MB_GROUPS, rep)
    a_head = -jnp.exp(mb_a_log.astype(f32)).reshape(MB_GROUPS, rep)
    y = ssd_scan(mx, mdt, a_head, mB, mC) + mx * mb_d.astype(f32).reshape(MB_GROUPS, rep)[..., None]
    y = rms_normalize(y.reshape(b, s, MB_INNER) * jax.nn.silu(mb_z)) * mb_norm_w.astype(f32)

    heads = jnp.concatenate([o_dn.reshape(b, s, -1), o_at.reshape(b, s, -1), y], -1).astype(x.dtype)
    return jnp.einsum('bse,ed->bsd', heads, w_out)


def swiglu(x, w1, w3, w2):
    return (jax.nn.silu(x @ w1) * (x @ w3)) @ w2


def moe_sublayer(x, router_w, router_b, exp_w1, exp_w3, exp_w2, sh_w1, sh_w3, sh_w2):
    b, s, d = x.shape
    t = b * s
    xt = x.reshape(t, d)
    scores = jax.nn.sigmoid(jnp.dot(xt.astype(jnp.float32), router_w.astype(jnp.float32)))
    _, sel = lax.top_k(scores + router_b.astype(jnp.float32), TOP_K)
    gate = jnp.take_along_axis(scores, sel, axis=-1)
    gate = gate / jnp.sum(gate, -1, keepdims=True) * ROUTED_SCALE
    flat_e = sel.reshape(-1)
    order = jnp.argsort(flat_e)
    e_sorted = flat_e[order]
    tok_sorted = order // TOP_K
    counts = jnp.bincount(flat_e, length=N_EXPERTS)
    padded = (counts + MOE_BLOCK - 1) // MOE_BLOCK * MOE_BLOCK
    pad_end = jnp.cumsum(padded)
    pad_start = pad_end - padded
    raw_start = jnp.cumsum(counts) - counts
    dest = pad_start[e_sorted] + jnp.arange(t * TOP_K) - raw_start[e_sorted]
    n_blocks = -(-(t * TOP_K) // MOE_BLOCK) + N_EXPERTS
    xbuf = jnp.zeros((n_blocks * MOE_BLOCK, d), x.dtype).at[dest].set(xt[tok_sorted])
    blk_expert = jnp.minimum(jnp.searchsorted(pad_end, jnp.arange(n_blocks) * MOE_BLOCK, side='right'),
                             N_EXPERTS - 1)

    def expert_block(args):
        xb, e = args
        return swiglu(xb, exp_w1[e], exp_w3[e], exp_w2[e])

    ybuf = lax.map(expert_block, (xbuf.reshape(n_blocks, MOE_BLOCK, d), blk_expert)).reshape(-1, d)
    y_sorted = ybuf[dest] * gate.reshape(-1)[order][:, None].astype(x.dtype)
    routed = jax.ops.segment_sum(y_sorted, tok_sorted, num_segments=t)
    return (routed + swiglu(xt, sh_w1, sh_w3, sh_w2)).reshape(b, s, d)


def _dt_bias(k, shape):
    dt = jnp.exp(jax.random.uniform(k, shape, jnp.float32, math.log(1e-3), math.log(1e-1)))
    return dt + jnp.log(-jnp.expm1(-dt))


def setup_inputs(seed: int = 0) -> dict:
    key = jax.random.key(seed)
    ks = jax.random.split(key, 32)
    f32 = jnp.float32
    L = DEPTH

    def nrm(k, shape, scale):
        return jax.random.normal(k, shape, f32) * scale

    return {
        'x': jax.random.normal(ks[0], (BATCH, SEQ, D_MODEL), f32),
        'w_in': nrm(ks[1], (L, D_MODEL, D_IN), D_MODEL ** -0.5),
        'dn_conv_w': nrm(ks[2], (L, DN_CONV, DN_QKV), DN_CONV ** -0.5),
        'dn_a_log': jnp.log(jax.random.uniform(ks[3], (L, DN_HEADS), f32, 1.0, 16.0)),
        'dn_dt_bias': _dt_bias(ks[4], (L, DN_HEADS)),
        'dn_norm_w': 1.0 + nrm(ks[5], (L, DN_DV), 0.01),
        'rel_bias': nrm(ks[6], (T5_BUCKETS, AT_HEADS), 0.5),
        'mb_conv_w': nrm(ks[7], (L, MB_CONV, MB_XBC), MB_CONV ** -0.5),
        'mb_conv_b': nrm(ks[8], (L, MB_XBC), 0.01),
        'mb_a_log': jnp.log(jax.random.uniform(ks[9], (L, MB_HEADS), f32, 1.0, 16.0)),
        'mb_dt_bias': _dt_bias(ks[10], (L, MB_HEADS)),
        'mb_d': 1.0 + nrm(ks[11], (L, MB_HEADS), 0.1),
        'mb_norm_w': 1.0 + nrm(ks[12], (L, MB_INNER), 0.01),
        'w_out': nrm(ks[13], (L, D_MIX, D_MODEL), D_MIX ** -0.5 * DEEPNORM_BETA),
        'ln1_g': 1.0 + nrm(ks[14], (L, D_MODEL), 0.01),
        'ln1_b': nrm(ks[15], (L, D_MODEL), 0.01),
        'router_w': nrm(ks[16], (L, D_MODEL, N_EXPERTS), D_MODEL ** -0.5),
        'router_b': nrm(ks[17], (L, N_EXPERTS), 0.01),
        'exp_w1': nrm(ks[18], (L, N_EXPERTS, D_MODEL, MOE_D_FF), D_MODEL ** -0.5),
        'exp_w3': nrm(ks[19], (L, N_EXPERTS, D_MODEL, MOE_D_FF), D_MODEL ** -0.5),
        'exp_w2': nrm(ks[20], (L, N_EXPERTS, MOE_D_FF, D_MODEL), MOE_D_FF ** -0.5 * DEEPNORM_BETA),
        'sh_w1': nrm(ks[21], (L, D_MODEL, SHARED_D_FF), D_MODEL ** -0.5),
        'sh_w3': nrm(ks[22], (L, D_MODEL, SHARED_D_FF), D_MODEL ** -0.5),
        'sh_w2': nrm(ks[23], (L, SHARED_D_FF, D_MODEL), SHARED_D_FF ** -0.5 * DEEPNORM_BETA),
        'ln2_g': 1.0 + nrm(ks[24], (L, D_MODEL), 0.01),
        'ln2_b': nrm(ks[25], (L, D_MODEL), 0.01),
    }


def reference(x, w_in, dn_conv_w, dn_a_log, dn_dt_bias, dn_norm_w, rel_bias,
              mb_conv_w, mb_conv_b, mb_a_log, mb_dt_bias, mb_d, mb_norm_w, w_out,
              ln1_g, ln1_b, router_w, router_b, exp_w1, exp_w3, exp_w2,
              sh_w1, sh_w3, sh_w2, ln2_g, ln2_b):
    for l in range(DEPTH):
        mix = mixer_sublayer(x, w_in[l], dn_conv_w[l], dn_a_log[l], dn_dt_bias[l], dn_norm_w[l],
                             rel_bias, mb_conv_w[l], mb_conv_b[l], mb_a_log[l], mb_dt_bias[l],
                             mb_d[l], mb_norm_w[l], w_out[l])
        x = layer_norm(DEEPNORM_ALPHA * x + mix, ln1_g[l], ln1_b[l])
        ffn = moe_sublayer(x, router_w[l], router_b[l], exp_w1[l], exp_w3[l], exp_w2[l],
                           sh_w1[l], sh_w3[l], sh_w2[l])
        x = layer_norm(DEEPNORM_ALPHA * x + ffn, ln2_g[l], ln2_b[l])
    return x
```

```python
import functools
import math

import numpy as np
import jax
import jax.numpy as jnp
from jax import lax
from jax.experimental import pallas as pl
from jax.experimental.pallas import tpu as pltpu

F32 = jnp.float32
BF16 = jnp.bfloat16

D_MODEL = 2048
DN_HEADS, DN_DK, DN_DV, DN_CONV = 4, 128, 128, 4
DN_QKV = DN_HEADS * (2 * DN_DK + DN_DV)
AT_HEADS, AT_DH = 4, 128
IX_HEADS, IX_DIM = 8, 64
DSA_TOPK = 256
T5_BUCKETS, T5_MAX_DIST = 32, 128
MB_HEADS, MB_HEADDIM, MB_GROUPS, MB_STATE, MB_CONV = 16, 64, 2, 128, 4
MB_INNER = MB_HEADS * MB_HEADDIM
MB_XBC = MB_INNER + 2 * MB_GROUPS * MB_STATE
D_MIX = DN_HEADS * DN_DV + AT_HEADS * AT_DH + MB_INNER
IN_SPLITS = (DN_QKV, DN_HEADS * DN_DV, DN_HEADS, DN_HEADS,
             AT_HEADS * AT_DH, AT_HEADS * AT_DH, AT_HEADS * AT_DH,
             IX_HEADS * IX_DIM, IX_DIM, IX_HEADS,
             MB_INNER, MB_XBC, MB_HEADS)
N_EXPERTS, TOP_K, MOE_D_FF, SHARED_D_FF = 64, 8, 384, 384
ROUTED_SCALE = 2.5
DEPTH = 4
DEEPNORM_ALPHA = (2 * DEPTH) ** 0.25
LN_EPS = 1e-5
RMS_EPS = 1e-6

LANES = 128
SUBLANES = 8
VMEM_LIMIT = 52 * 1024 * 1024

_SRC = dict(zip(
    ("dn_qkv", "dn_z", "dn_b", "dn_a", "at_q", "at_k", "at_v", "ix_q", "ix_k", "ix_w",
     "mb_z", "mb_xbc", "mb_dt"),
    zip(np.cumsum((0,) + IN_SPLITS[:-1]).tolist(), IN_SPLITS)))
_BIG_ORDER = ("dn_qkv", "dn_z", "at_q", "at_k", "at_v", "ix_q", "mb_z", "mb_xbc")
_SMALL_ORDER = ("ix_k", "dn_b", "dn_a", "ix_w", "mb_dt")


def _layout(order):
    off, out = 0, {}
    for name in order:
        out[name] = off
        off += _SRC[name][1]
    return out, off


BIG, N_BIG = _layout(_BIG_ORDER)
SMALL, _N_SMALL_USED = _layout(_SMALL_ORDER)
N_SMALL = LANES
_PERM_BIG = np.concatenate([np.arange(_SRC[n][0], _SRC[n][0] + _SRC[n][1]) for n in _BIG_ORDER])
_PERM_SMALL = np.concatenate([np.arange(_SRC[n][0], _SRC[n][0] + _SRC[n][1]) for n in _SMALL_ORDER])

SEQ_BLK = 128
DSA_TQ = 128
DSA_KC = 512
MM_TM, MM_TN = 1024, 512
ROW_TM = 512
MOE_BM = 512
INT_MIN = -2 ** 31
NEG_BIG = -1e30


def _cparams(*sem):
    return pltpu.CompilerParams(dimension_semantics=sem, vmem_limit_bytes=VMEM_LIMIT)


def _sigmoid(x):
    return 1.0 / (1.0 + jnp.exp(-x))


def _silu(x):
    return x * _sigmoid(x)


def _softplus(x):
    return jnp.maximum(x, 0.0) + jnp.log(1.0 + jnp.exp(-jnp.abs(x)))


def _dot(a, b):
    return jnp.dot(a, b, preferred_element_type=F32)


def _dot_nt(a, b):
    return lax.dot_general(a, b, (((1,), (1,)), ((), ())), preferred_element_type=F32)


def _split(a):
    hi = a.astype(BF16)
    lo = (a - hi.astype(F32)).astype(BF16)
    return hi, lo


def _dot_acc(a, b):
    ah, al = _split(a)
    bh, bl = _split(b)
    return _dot(ah, bh) + _dot(ah, bl) + _dot(al, bh)


def _dot_acc_exact_rhs(a, b_bf16):
    ah, al = _split(a)
    return _dot(ah, b_bf16) + _dot(al, b_bf16)


def _cumsum_rows(tri_bf16, val):
    hi, lo = _split(val)
    return _dot(tri_bf16, hi) + _dot(tri_bf16, lo)


def _causal_conv_silu(raw, tail_ref, cw, bias):
    rows = raw.shape[0]
    xx = jnp.concatenate([tail_ref[...], raw], axis=0)
    acc = raw * cw[DN_CONV - 1:DN_CONV, :]
    for j in range(1, DN_CONV):
        acc = acc + pltpu.roll(xx, j, 0)[SUBLANES:, :] * cw[DN_CONV - 1 - j:DN_CONV - j, :]
    tail_ref[...] = raw[rows - SUBLANES:, :]
    if bias is not None:
        acc = acc + bias
    return _silu(acc)


def _tri_masks(n):
    r = lax.broadcasted_iota(jnp.int32, (n, n), 0)
    c = lax.broadcasted_iota(jnp.int32, (n, n), 1)
    return r >= c, r > c, r == c


def _mm_kernel(x_ref, w_ref, o_ref):
    o_ref[...] = _dot(x_ref[...], w_ref[...]).astype(o_ref.dtype)


def _matmul(x, w, out_dtype, tm, tn):
    m, k = x.shape
    n = w.shape[1]
    return pl.pallas_call(
        _mm_kernel,
        grid=(m // tm, n // tn),
        in_specs=[pl.BlockSpec((tm, k), lambda i, j: (i, 0)),
                  pl.BlockSpec((k, tn), lambda i, j: (0, j))],
        out_specs=pl.BlockSpec((tm, tn), lambda i, j: (i, j)),
        out_shape=jax.ShapeDtypeStruct((m, n), out_dtype),
        compiler_params=_cparams("parallel", "arbitrary"),
        name="in_proj",
    )(x, w)


def _dn_kernel(q_ref, k_ref, v_ref, z_ref, sm_ref, cw_ref, alog_ref, dtb_ref, nw_ref,
               o_ref, tail_ref, st_ref):
    c = SEQ_BLK

    @pl.when(pl.program_id(1) == 0)
    def _():
        tail_ref[...] = jnp.zeros_like(tail_ref)
        st_ref[...] = jnp.zeros_like(st_ref)

    raw = jnp.concatenate([q_ref[...], k_ref[...], v_ref[...]], axis=1).astype(F32)
    qkv = _causal_conv_silu(raw, tail_ref, cw_ref[...], None)

    sm = sm_ref[...]
    beta_all = _sigmoid(sm)
    g_all = -jnp.exp(alog_ref[...]) * _softplus(sm + dtb_ref[...])
    incl, strict, diag = _tri_masks(c)
    tri = jnp.where(incl, 1.0, 0.0).astype(BF16)
    gc_all = _cumsum_rows(tri, g_all)
    gc_t = gc_all.T
    eye = jnp.where(diag, 1.0, 0.0)

    for h in range(DN_HEADS):
        lo = h * DN_DK
        q = qkv[:, lo:lo + DN_DK]
        k = qkv[:, DN_HEADS * DN_DK + lo:DN_HEADS * DN_DK + lo + DN_DK]
        v = qkv[:, 2 * DN_HEADS * DN_DK + lo:2 * DN_HEADS * DN_DK + lo + DN_DV]
        q = q * lax.rsqrt(jnp.sum(q * q, -1, keepdims=True) + RMS_EPS) * DN_DK ** -0.5
        k = k * lax.rsqrt(jnp.sum(k * k, -1, keepdims=True) + RMS_EPS)
        bt = beta_all[:, SMALL["dn_b"] + h:SMALL["dn_b"] + h + 1]
        gc = gc_all[:, SMALL["dn_a"] + h:SMALL["dn_a"] + h + 1]
        gr = gc_t[SMALL["dn_a"] + h:SMALL["dn_a"] + h + 1, :]
        decay = jnp.exp(jnp.where(incl, gc - gr, NEG_BIG))
        egc = jnp.exp(gc)
        g_last = gc[c - 1:c, :]
        e_to_end = jnp.exp(g_last - gc)
        g_tot = jnp.exp(g_last)

        kb = k * bt
        kbf = k.astype(BF16)
        a_mat = jnp.where(strict, _dot_nt(kb.astype(BF16), kbf) * decay, 0.0)
        inv = eye - a_mat
        pw = _dot_acc(a_mat, a_mat)
        n_fac = int(math.log2(c)) - 1
        for i in range(n_fac):
            inv = inv + _dot_acc(inv, pw)
            if i + 1 < n_fac:
                pw = _dot_acc(pw, pw)
        rhs = jnp.concatenate([v * bt, kb * egc], axis=1).astype(BF16)
        sol = _dot(inv.astype(BF16), rhs)
        u, w = sol[:, :DN_DV], sol[:, DN_DV:]
        qk = _dot_nt(q.astype(BF16), kbf) * decay

        state = st_ref[h]
        sb = state.astype(BF16)
        v_new = u - _dot(w.astype(BF16), sb)
        vnb = v_new.astype(BF16)
        o = _dot((q * egc).astype(BF16), sb) + _dot(qk.astype(BF16), vnb)
        k_out_t = (k * e_to_end).T.astype(BF16)
        st_ref[h] = state * g_tot + _dot(k_out_t, vnb)

        zz = z_ref[:, lo:lo + DN_DV].astype(F32)
        on = o * lax.rsqrt(jnp.mean(o * o, -1, keepdims=True) + RMS_EPS) * nw_ref[...] * _silu(zz)
        o_ref[:, lo:lo + DN_DV] = on.astype(o_ref.dtype)


def _deltanet(big, small, conv_w, alog_row, dtb_row, norm_w, batch, seq):
    ns = seq // SEQ_BLK
    t = batch * seq
    w512 = DN_HEADS * DN_DK

    def rows(col_blk):
        return lambda b, s: (b * ns + s, col_blk)

    return pl.pallas_call(
        _dn_kernel,
        grid=(batch, ns),
        in_specs=[pl.BlockSpec((SEQ_BLK, w512), rows(BIG["dn_qkv"] // w512)),
                  pl.BlockSpec((SEQ_BLK, w512), rows(BIG["dn_qkv"] // w512 + 1)),
                  pl.BlockSpec((SEQ_BLK, w512), rows(BIG["dn_qkv"] // w512 + 2)),
                  pl.BlockSpec((SEQ_BLK, w512), rows(BIG["dn_z"] // w512)),
                  pl.BlockSpec((SEQ_BLK, N_SMALL), rows(0)),
                  pl.BlockSpec((DN_CONV, DN_QKV), lambda b, s: (0, 0)),
                  pl.BlockSpec((1, N_SMALL), lambda b, s: (0, 0)),
                  pl.BlockSpec((1, N_SMALL), lambda b, s: (0, 0)),
                  pl.BlockSpec((1, DN_DV), lambda b, s: (0, 0))],
        out_specs=pl.BlockSpec((SEQ_BLK, w512), rows(0)),
        out_shape=jax.ShapeDtypeStruct((t, w512), BF16),
        scratch_shapes=[pltpu.VMEM((SUBLANES, DN_QKV), F32),
                        pltpu.VMEM((DN_HEADS, DN_DK, DN_DV), F32)],
        compiler_params=_cparams("parallel", "arbitrary"),
        name="deltanet",
    )(big, big, big, big, small, conv_w, alog_row, dtb_row, norm_w)


def _ssd_kernel(x_ref, bc_ref, z_ref, sm_ref, cw_ref, cb_ref, alog_ref, dtb_ref, dx_ref, nw_ref,
                ex_ref, o_ref, tail_ref, st_ref):
    c = SEQ_BLK
    hpg = MB_HEADS // MB_GROUPS
    gw = hpg * MB_HEADDIM

    @pl.when(pl.program_id(1) == 0)
    def _():
        tail_ref[...] = jnp.zeros_like(tail_ref)
        st_ref[...] = jnp.zeros_like(st_ref)

    raw = jnp.concatenate([x_ref[...], bc_ref[...]], axis=1).astype(F32)
    xbc = _causal_conv_silu(raw, tail_ref, cw_ref[...], cb_ref[...])
    mx = xbc[:, :MB_INNER]
    m_b = xbc[:, MB_INNER:MB_INNER + MB_GROUPS * MB_STATE]
    m_c = xbc[:, MB_INNER + MB_GROUPS * MB_STATE:]

    lane = lax.broadcasted_iota(jnp.int32, (1, N_SMALL), 1)
    dt_lanes = (lane >= SMALL["mb_dt"]) & (lane < SMALL["mb_dt"] + MB_HEADS)
    sm = sm_ref[...]
    dt_all = jnp.where(dt_lanes, _softplus(sm + dtb_ref[...]), 0.0)
    la_all = dt_all * (-jnp.exp(alog_ref[...]))
    incl, _, _ = _tri_masks(c)
    tri = jnp.where(incl, 1.0, 0.0).astype(BF16)
    cs_all = _cumsum_rows(tri, la_all)
    cs_t = cs_all.T
    cs_last = cs_all[c - 1:c, :]
    ex = ex_ref[...]

    def expand(val):
        return _dot_acc_exact_rhs(jnp.where(dt_lanes, val, 0.0), ex)

    dtx = expand(dt_all)
    ecs = expand(jnp.exp(cs_all))
    eto = expand(jnp.exp(cs_last - cs_all))
    dec = ecs[c - 1:c, :]
    xd = mx * dtx
    xde = xd * eto
    lane_c = lax.broadcasted_iota(jnp.int32, (1, LANES), 1)
    first_half = lane_c < MB_HEADDIM

    pieces = []
    for g in range(MB_GROUPS):
        bg = m_b[:, g * MB_STATE:(g + 1) * MB_STATE]
        cg = m_c[:, g * MB_STATE:(g + 1) * MB_STATE].astype(BF16)
        cb = _dot_nt(cg, bg.astype(BF16))
        h_t = st_ref[g]
        y_off = _dot(cg, h_t.astype(BF16)) * ecs[:, g * gw:(g + 1) * gw]
        for pr in range(hpg // 2):
            mats = []
            for hd in (g * hpg + 2 * pr, g * hpg + 2 * pr + 1):
                col = SMALL["mb_dt"] + hd
                lmat = jnp.exp(jnp.where(incl, cs_all[:, col:col + 1] - cs_t[col:col + 1, :], NEG_BIG))
                mats.append((cb * lmat).astype(BF16))
            lhs = jnp.concatenate(mats, axis=1)
            lo = g * gw + pr * LANES
            xp = xd[:, lo:lo + LANES]
            rhs = jnp.concatenate([jnp.where(first_half, xp, 0.0),
                                   jnp.where(first_half, 0.0, xp)], axis=0).astype(BF16)
            pieces.append(_dot(lhs, rhs) + y_off[:, pr * LANES:(pr + 1) * LANES])
        st_ref[g] = (h_t * dec[:, g * gw:(g + 1) * gw]
                     + _dot(bg.T.astype(BF16), xde[:, g * gw:(g + 1) * gw].astype(BF16)))

    y = jnp.concatenate(pieces, axis=1) + mx * dx_ref[...]
    yz = y * _silu(z_ref[...].astype(F32))
    out = yz * lax.rsqrt(jnp.mean(yz * yz, -1, keepdims=True) + RMS_EPS) * nw_ref[...]
    o_ref[...] = out.astype(o_ref.dtype)


def _ssd(big, small, conv_w, conv_b, alog_row, dtb_row, d_row, norm_w, expand_mat, batch, seq):
    ns = seq // SEQ_BLK
    t = batch * seq
    bcw = 2 * MB_GROUPS * MB_STATE

    def rows(col_blk):
        return lambda b, s: (b * ns + s, col_blk)

    const = lambda b, s: (0, 0)
    return pl.pallas_call(
        _ssd_kernel,
        grid=(batch, ns),
        in_specs=[pl.BlockSpec((SEQ_BLK, MB_INNER), rows(BIG["mb_xbc"] // MB_INNER)),
                  pl.BlockSpec((SEQ_BLK, bcw), rows((BIG["mb_xbc"] + MB_INNER) // bcw)),
                  pl.BlockSpec((SEQ_BLK, MB_INNER), rows(BIG["mb_z"] // MB_INNER)),
                  pl.BlockSpec((SEQ_BLK, N_SMALL), rows(0)),
                  pl.BlockSpec((MB_CONV, MB_XBC), const),
                  pl.BlockSpec((1, MB_XBC), const),
                  pl.BlockSpec((1, N_SMALL), const),
                  pl.BlockSpec((1, N_SMALL), const),
                  pl.BlockSpec((1, MB_INNER), const),
                  pl.BlockSpec((1, MB_INNER), const),
                  pl.BlockSpec((N_SMALL, MB_INNER), const)],
        out_specs=pl.BlockSpec((SEQ_BLK, MB_INNER), rows(0)),
        out_shape=jax.ShapeDtypeStruct((t, MB_INNER), BF16),
        scratch_shapes=[pltpu.VMEM((SUBLANES, MB_XBC), F32),
                        pltpu.VMEM((MB_GROUPS, MB_STATE, MB_INNER // MB_GROUPS), F32)],
        compiler_params=_cparams("parallel", "arbitrary"),
        name="ssd",
    )(big, big, big, small, conv_w, conv_b, alog_row, dtb_row, d_row, norm_w, expand_mat)


def _t5_bucket_np(d):
    exact = T5_BUCKETS // 2
    dd = np.maximum(d, exact).astype(np.float64)
    big = exact + (np.log(dd / exact) / math.log(T5_MAX_DIST / exact) * (T5_BUCKETS - exact)).astype(np.int64)
    return np.where(d < exact, d, np.minimum(big, T5_BUCKETS - 1))


_T5_FAR = int(np.max(np.nonzero(_t5_bucket_np(np.arange(1 << 16)) != T5_BUCKETS - 1)[0])) + 1
_FAR_BLK = -(-_T5_FAR // DSA_TQ) * DSA_TQ
_NEAR_BACK = DSA_KC + _FAR_BLK - DSA_TQ
_STRIP_W = _NEAR_BACK + DSA_KC


def _t5_bucket(dist):
    exact = T5_BUCKETS // 2
    d = jnp.maximum(dist, exact).astype(F32)
    big = exact + (jnp.log(d / exact) / math.log(T5_MAX_DIST / exact)
                   * (T5_BUCKETS - exact)).astype(jnp.int32)
    return jnp.where(dist < exact, dist, jnp.minimum(big, T5_BUCKETS - 1))


def _dsa_kernel(n_sel, seq, far_ref, q_ref, ixq_ref, smq_ref, k_ref, v_ref, smk_ref, strip_ref,
                o_ref, keys_ref, tie_ref):
    tq, kc = DSA_TQ, DSA_KC
    nsub = kc // LANES
    t0 = pl.program_id(1) * tq
    n_chunks = t0 // kc + 1
    n_far = jnp.maximum(t0 - _FAR_BLK, 0) // kc
    row_pos = t0 + lax.broadcasted_iota(jnp.int32, (tq, 1), 0)
    lane = lax.broadcasted_iota(jnp.int32, (1, LANES), 1)
    low_half = lane < IX_DIM

    smq = smq_ref[...]
    ixq = ixq_ref[...]
    zero_b = jnp.zeros((), BF16)
    q_heads, w_cols = [], []
    for h in range(IX_HEADS):
        pair = ixq[:, (h // 2) * LANES:(h // 2 + 1) * LANES]
        keep = low_half if h % 2 == 0 else jnp.logical_not(low_half)
        q_heads.append(jnp.where(keep, pair, zero_b))
        w_cols.append(smq[:, SMALL["ix_w"] + h:SMALL["ix_w"] + h + 1])

    def score_chunk(ci, carry):
        s0 = pl.multiple_of(ci * kc, kc)
        kic = smk_ref[pl.ds(s0, kc), :]
        klo = jnp.where(low_half, kic, 0.0)
        ki2 = (klo + pltpu.roll(klo, IX_DIM, 1)).astype(BF16)
        sc = jnp.zeros((tq, kc), F32)
        for h in range(IX_HEADS):
            sc = sc + w_cols[h] * jnp.maximum(_dot_nt(q_heads[h], ki2), 0.0)
        bits = pltpu.bitcast(sc, jnp.int32)
        key = jnp.where(bits >= 0, bits, bits ^ jnp.int32(0x7FFFFFFF))
        col_pos = s0 + lax.broadcasted_iota(jnp.int32, (1, kc), 1)
        keys_ref[:, pl.ds(s0, kc)] = jnp.where(col_pos <= row_pos, key, jnp.int32(INT_MIN))
        return carry

    lax.fori_loop(0, n_chunks, score_chunk, 0)

    def count(pred):
        def body(ci, acc):
            s0 = pl.multiple_of(ci * kc, kc)
            kk = keys_ref[:, pl.ds(s0, kc)]
            for j in range(nsub):
                acc = acc + jnp.where(pred(kk[:, j * LANES:(j + 1) * LANES], s0 + j * LANES), 1.0, 0.0)
            return acc
        acc = lax.fori_loop(0, n_chunks, body, jnp.zeros((tq, LANES), F32))
        return jnp.sum(acc, axis=1, keepdims=True)

    k_sel = float(n_sel)
    cnt0 = count(lambda kk, c0: kk >= 0)
    thr = jnp.broadcast_to(jnp.where(cnt0 >= k_sel, 0, INT_MIN).astype(jnp.int32), (tq, LANES))

    def bit_step(i, thr):
        cand = thr + jnp.left_shift(jnp.int32(1), 30 - i)
        cnt = count(lambda kk, c0: kk >= cand)
        return jnp.where(cnt >= k_sel, cand, thr)

    thr = lax.fori_loop(0, 31, bit_step, thr)

    cnt_gt = count(lambda kk, c0: kk > thr)
    cnt_eq = count(lambda kk, c0: kk == thr)
    need = k_sel - cnt_gt
    has_tie = (cnt_eq > need) & (thr[:, :1] > INT_MIN)
    tie_ref[...] = jnp.full((tq, LANES), seq, jnp.int32)

    @pl.when(jnp.max(jnp.where(has_tie, 1.0, 0.0)) > 0.0)
    def _():
        def idx_step(i, pos):
            cand = pos + jnp.left_shift(jnp.int32(1), int(math.ceil(math.log2(seq))) - 1 - i)
            sub_lane = lax.broadcasted_iota(jnp.int32, (1, LANES), 1)
            cnt = count(lambda kk, c0: (kk == thr) & (c0 + sub_lane < cand))
            return jnp.where(cnt < need, cand, pos)
        pos = lax.fori_loop(0, int(math.ceil(math.log2(seq))), idx_step,
                            jnp.zeros((tq, LANES), jnp.int32))
        tie_ref[...] = pos

    tie = tie_ref[...]
    thr_w = jnp.concatenate([thr] * nsub, axis=1)
    tie_w = jnp.concatenate([tie] * nsub, axis=1)

    scale = AT_DH ** -0.5
    qh = [q_ref[:, h * AT_DH:(h + 1) * AT_DH] for h in range(AT_HEADS)]

    def attend(ci, carry, near):
        s0 = pl.multiple_of(ci * kc, kc)
        kk = keys_ref[:, pl.ds(s0, kc)]
        col_pos = s0 + lax.broadcasted_iota(jnp.int32, (1, kc), 1)
        sel = (col_pos <= row_pos) & ((kk > thr_w) | ((kk == thr_w) & (col_pos <= tie_w)))
        madd = jnp.where(sel, 0.0, NEG_BIG)
        kb = k_ref[pl.ds(s0, kc), :]
        vb = v_ref[pl.ds(s0, kc), :]
        out = []
        for h in range(AT_HEADS):
            m_old, l_old, acc = carry[h]
            s = _dot_nt(qh[h], kb[:, h * AT_DH:(h + 1) * AT_DH]) * scale + madd
            if near:
                m0 = pl.multiple_of(s0 - t0 + _NEAR_BACK, LANES)
                s = s + strip_ref[h, :, pl.ds(m0, kc)]
            else:
                s = s + far_ref[h]
            m_new = jnp.maximum(m_old, jnp.max(s, axis=1, keepdims=True))
            alpha = jnp.exp(m_old - m_new)
            p = jnp.exp(s - m_new)
            l_new = alpha * l_old + jnp.sum(p, axis=1, keepdims=True)
            acc = alpha * acc + _dot(p.astype(BF16), vb[:, h * AT_DH:(h + 1) * AT_DH])
            out.append((m_new, l_new, acc))
        return tuple(out)

    init = tuple((jnp.full((tq, 1), NEG_BIG, F32), jnp.zeros((tq, 1), F32),
                  jnp.zeros((tq, AT_DH), F32)) for _ in range(AT_HEADS))
    carry = lax.fori_loop(0, n_far, functools.partial(attend, near=False), init)
    carry = lax.fori_loop(n_far, n_chunks, functools.partial(attend, near=True), carry)
    for h in range(AT_HEADS):
        _, l_fin, acc = carry[h]
        o_ref[:, h * AT_DH:(h + 1) * AT_DH] = (acc / l_fin).astype(o_ref.dtype)


def _dsa(big, small, strip, far_bias, batch, seq):
    nq = seq // DSA_TQ
    t = batch * seq
    w512 = AT_HEADS * AT_DH
    n_sel = min(DSA_TOPK, seq // 4)
    assert seq % DSA_KC == 0 and DSA_KC >= n_sel

    def qrows(col_blk):
        return lambda b, i: (b * nq + i, col_blk)

    def krows(col_blk):
        return lambda b, i: (b, col_blk)

    return pl.pallas_call(
        functools.partial(_dsa_kernel, n_sel, seq),
        grid=(batch, nq),
        in_specs=[pl.BlockSpec(memory_space=pltpu.SMEM),
                  pl.BlockSpec((DSA_TQ, w512), qrows(BIG["at_q"] // w512)),
                  pl.BlockSpec((DSA_TQ, w512), qrows(BIG["ix_q"] // w512)),
                  pl.BlockSpec((DSA_TQ, N_SMALL), qrows(0)),
                  pl.BlockSpec((seq, w512), krows(BIG["at_k"] // w512)),
                  pl.BlockSpec((seq, w512), krows(BIG["at_v"] // w512)),
                  pl.BlockSpec((seq, N_SMALL), krows(0)),
                  pl.BlockSpec((AT_HEADS, DSA_TQ, _STRIP_W), lambda b, i: (0, 0, 0))],
        out_specs=pl.BlockSpec((DSA_TQ, w512), qrows(0)),
        out_shape=jax.ShapeDtypeStruct((t, w512), BF16),
        scratch_shapes=[pltpu.VMEM((DSA_TQ, seq), jnp.int32),
                        pltpu.VMEM((DSA_TQ, LANES), jnp.int32)],
        compiler_params=_cparams("parallel", "arbitrary"),
        name="dsa",
    )(far_bias, big, big, small, big, big, small, strip)


def _dsa_bias_tables(rel_bias):
    dist = (np.arange(DSA_TQ)[:, None] + _NEAR_BACK - np.arange(_STRIP_W)[None, :])
    table = rel_bias.astype(F32)[_t5_bucket(jnp.arange(_NEAR_BACK + DSA_TQ))]
    strip = jnp.where(jnp.asarray(dist >= 0)[..., None], table[np.maximum(dist, 0)], 0.0)
    far = rel_bias.astype(F32)[_t5_bucket(jnp.asarray([_T5_FAR]))][0]
    return jnp.moveaxis(strip, -1, 0), far


def _layer_norm(r, g, b):
    rc = r - jnp.mean(r, -1, keepdims=True)
    var = jnp.mean(rc * rc, -1, keepdims=True)
    return rc * lax.rsqrt(var + LN_EPS) * g + b


def _outproj_kernel(odn_ref, oat_ref, y_ref, x_ref, w_ref, g_ref, b_ref, rwh_ref, rwl_ref,
                    x1_ref, x1b_ref, sc_ref):
    n_dn = DN_HEADS * DN_DV
    n_at = AT_HEADS * AT_DH
    mix = (_dot(odn_ref[...], w_ref[:n_dn, :])
           + _dot(oat_ref[...], w_ref[n_dn:n_dn + n_at, :])
           + _dot(y_ref[...], w_ref[n_dn + n_at:, :]))
    x1 = _layer_norm(DEEPNORM_ALPHA * x_ref[...] + mix, g_ref[...], b_ref[...])
    x1_ref[...] = x1
    x1b_ref[...] = x1.astype(BF16)
    xh, xl = _split(x1)
    logits = _dot(xh, rwh_ref[...]) + _dot(xh, rwl_ref[...]) + _dot(xl, rwh_ref[...])
    sc_ref[...] = _sigmoid(logits)


def _outproj(o_dn, o_at, y, x, w_out, g, b, rw_hi, rw_lo):
    t = x.shape[0]
    tm = min(ROW_TM, t)
    row = lambda i: (i, 0)
    const = lambda i: (0, 0)
    return pl.pallas_call(
        _outproj_kernel,
        grid=(t // tm,),
        in_specs=[pl.BlockSpec((tm, o_dn.shape[1]), row),
                  pl.BlockSpec((tm, o_at.shape[1]), row),
                  pl.BlockSpec((tm, y.shape[1]), row),
                  pl.BlockSpec((tm, D_MODEL), row),
                  pl.BlockSpec((D_MIX, D_MODEL), const),
                  pl.BlockSpec((1, D_MODEL), const),
                  pl.BlockSpec((1, D_MODEL), const),
                  pl.BlockSpec((D_MODEL, LANES), const),
                  pl.BlockSpec((D_MODEL, LANES), const)],
        out_specs=[pl.BlockSpec((tm, D_MODEL), row),
                   pl.BlockSpec((tm, D_MODEL), row),
                   pl.BlockSpec((tm, LANES), row)],
        out_shape=[jax.ShapeDtypeStruct((t, D_MODEL), F32),
                   jax.ShapeDtypeStruct((t, D_MODEL), BF16),
                   jax.ShapeDtypeStruct((t, LANES), F32)],
        compiler_params=_cparams("parallel"),
        name="out_proj_ln_router",
    )(o_dn, o_at, y, x, w_out, g, b, rw_hi, rw_lo)


def _expert_kernel(be_ref, nu_ref, x_ref, gate_ref, w1_ref, w3_ref, w2_ref, o_ref):
    i = pl.program_id(0)

    @pl.when(i < nu_ref[0])
    def _():
        x = x_ref[...]
        h = _silu(_dot(x, w1_ref[...])) * _dot(x, w3_ref[...]) * gate_ref[...]
        o_ref[...] = _dot(h.astype(BF16), w2_ref[...]).astype(o_ref.dtype)

    @pl.when(i >= nu_ref[0])
    def _():
        o_ref[...] = jnp.zeros_like(o_ref)


def _experts(blk_expert, n_used, xbuf, gate_buf, w1, w3, w2):
    n_rows = xbuf.shape[0]
    n_blocks = n_rows // MOE_BM
    grid_spec = pltpu.PrefetchScalarGridSpec(
        num_scalar_prefetch=2,
        grid=(n_blocks,),
        in_specs=[pl.BlockSpec((MOE_BM, D_MODEL), lambda i, be, nu: (i, 0)),
                  pl.BlockSpec((MOE_BM, 1), lambda i, be, nu: (i, 0)),
                  pl.BlockSpec((None, D_MODEL, MOE_D_FF), lambda i, be, nu: (be[i], 0, 0)),
                  pl.BlockSpec((None, D_MODEL, MOE_D_FF), lambda i, be, nu: (be[i], 0, 0)),
                  pl.BlockSpec((None, MOE_D_FF, D_MODEL), lambda i, be, nu: (be[i], 0, 0))],
        out_specs=pl.BlockSpec((MOE_BM, D_MODEL), lambda i, be, nu: (i, 0)),
    )
    return pl.pallas_call(
        _expert_kernel,
        grid_spec=grid_spec,
        out_shape=jax.ShapeDtypeStruct((n_rows, D_MODEL), BF16),
        compiler_params=_cparams("arbitrary"),
        name="experts",
    )(blk_expert, n_used, xbuf, gate_buf, w1, w3, w2)


def _shared_kernel(x_ref, xb_ref, routed_ref, w1_ref, w3_ref, w2_ref, g_ref, b_ref, x2_ref, x2b_ref):
    xb = xb_ref[...]
    h = _silu(_dot(xb, w1_ref[...])) * _dot(xb, w3_ref[...])
    ffn = _dot(h.astype(BF16), w2_ref[...]) + routed_ref[...].astype(F32)
    x2 = _layer_norm(DEEPNORM_ALPHA * x_ref[...] + ffn, g_ref[...], b_ref[...])
    x2_ref[...] = x2
    x2b_ref[...] = x2.astype(BF16)


def _shared(x1, x1b, routed, w1, w3, w2, g, b):
    t = x1.shape[0]
    tm = min(ROW_TM, t)
    row = lambda i: (i, 0)
    const = lambda i: (0, 0)
    return pl.pallas_call(
        _shared_kernel,
        grid=(t // tm,),
        in_specs=[pl.BlockSpec((tm, D_MODEL), row),
                  pl.BlockSpec((tm, D_MODEL), row),
                  pl.BlockSpec((tm, D_MODEL), row),
                  pl.BlockSpec((D_MODEL, SHARED_D_FF), const),
                  pl.BlockSpec((D_MODEL, SHARED_D_FF), const),
                  pl.BlockSpec((SHARED_D_FF, D_MODEL), const),
                  pl.BlockSpec((1, D_MODEL), const),
                  pl.BlockSpec((1, D_MODEL), const)],
        out_specs=[pl.BlockSpec((tm, D_MODEL), row),
                   pl.BlockSpec((tm, D_MODEL), row)],
        out_shape=[jax.ShapeDtypeStruct((t, D_MODEL), F32),
                   jax.ShapeDtypeStruct((t, D_MODEL), BF16)],
        compiler_params=_cparams("parallel"),
        name="shared_ln",
    )(x1, x1b, routed, w1, w3, w2, g, b)


def _route(scores, router_b, x1b):
    t = scores.shape[0]
    _, sel = lax.top_k(scores + router_b.astype(F32), TOP_K)
    gate = jnp.take_along_axis(scores, sel, axis=-1)
    gate = gate / jnp.sum(gate, -1, keepdims=True) * ROUTED_SCALE
    flat_e = sel.reshape(-1)
    order = jnp.argsort(flat_e)
    e_sorted = flat_e[order]
    tok_sorted = (order // TOP_K).astype(jnp.int32)
    counts = jnp.bincount(flat_e, length=N_EXPERTS)
    padded = (counts + MOE_BM - 1) // MOE_BM * MOE_BM
    pad_end = jnp.cumsum(padded)
    pad_start = pad_end - padded
    raw_start = jnp.cumsum(counts) - counts
    dest = (pad_start[e_sorted] + jnp.arange(t * TOP_K) - raw_start[e_sorted]).astype(jnp.int32)
    n_blocks = -(-(t * TOP_K) // MOE_BM) + N_EXPERTS
    n_rows = n_blocks * MOE_BM
    src_tok = jnp.zeros((n_rows,), jnp.int32).at[dest].set(tok_sorted)
    gate_buf = jnp.zeros((n_rows,), F32).at[dest].set(gate.reshape(-1)[order])
    blk_expert = jnp.minimum(
        jnp.searchsorted(pad_end, jnp.arange(n_blocks) * MOE_BM, side='right'),
        N_EXPERTS - 1).astype(jnp.int32)
    n_used = (pad_end[-1] // MOE_BM).astype(jnp.int32).reshape(1)
    pos = jnp.zeros((t * TOP_K,), jnp.int32).at[order].set(dest)
    xbuf = jnp.take(x1b, src_tok, axis=0)
    return xbuf, gate_buf[:, None], blk_expert, n_used, pos


def _lane_row(values, offset):
    return jnp.zeros((1, N_SMALL), F32).at[0, offset:offset + values.shape[0]].set(values.astype(F32))


def kernel(x, w_in, dn_conv_w, dn_a_log, dn_dt_bias, dn_norm_w, rel_bias, mb_conv_w, mb_conv_b,
           mb_a_log, mb_dt_bias, mb_d, mb_norm_w, w_out, ln1_g, ln1_b, router_w, router_b,
           exp_w1, exp_w3, exp_w2, sh_w1, sh_w3, sh_w2, ln2_g, ln2_b):
    batch, seq, d = x.shape
    t = batch * seq
    depth = w_in.shape[0]
    strip, far_bias = _dsa_bias_tables(rel_bias)
    expand_np = np.zeros((N_SMALL, MB_INNER), np.float32)
    for hd in range(MB_HEADS):
        expand_np[SMALL["mb_dt"] + hd, hd * MB_HEADDIM:(hd + 1) * MB_HEADDIM] = 1.0
    expand_mat = jnp.asarray(expand_np, BF16)

    xf = x.reshape(t, d).astype(F32)
    xb = xf.astype(BF16)
    for l in range(depth):
        w_l = w_in[l]
        w_big = w_l[:, _PERM_BIG].astype(BF16)
        w_small = jnp.pad(w_l[:, _PERM_SMALL], ((0, 0), (0, N_SMALL - _PERM_SMALL.size))).astype(BF16)
        big = _matmul(xb, w_big, BF16, min(MM_TM, t), MM_TN)
        small = _matmul(xb, w_small, F32, min(MM_TM, t), N_SMALL)

        o_dn = _deltanet(big, small, dn_conv_w[l].astype(F32),
                         _lane_row(dn_a_log[l], SMALL["dn_a"]), _lane_row(dn_dt_bias[l], SMALL["dn_a"]),
                         dn_norm_w[l].astype(F32)[None, :], batch, seq)
        o_at = _dsa(big, small, strip, far_bias, batch, seq)
        y_mb = _ssd(big, small, mb_conv_w[l].astype(F32), mb_conv_b[l].astype(F32)[None, :],
                    _lane_row(mb_a_log[l], SMALL["mb_dt"]), _lane_row(mb_dt_bias[l], SMALL["mb_dt"]),
                    jnp.repeat(mb_d[l].astype(F32), MB_HEADDIM)[None, :],
                    mb_norm_w[l].astype(F32)[None, :], expand_mat, batch, seq)

        rw = jnp.pad(router_w[l].astype(F32), ((0, 0), (0, LANES - N_EXPERTS)))
        rw_hi = rw.astype(BF16)
        rw_lo = (rw - rw_hi.astype(F32)).astype(BF16)
        x1, x1b, scores = _outproj(o_dn, o_at, y_mb, xf, w_out[l].astype(BF16),
                                   ln1_g[l].astype(F32)[None, :], ln1_b[l].astype(F32)[None, :],
                                   rw_hi, rw_lo)

        xbuf, gate_buf, blk_expert, n_used, pos = _route(scores[:, :N_EXPERTS], router_b[l], x1b)
        ybuf = _experts(blk_expert, n_used, xbuf, gate_buf, exp_w1[l].astype(BF16),
                        exp_w3[l].astype(BF16), exp_w2[l].astype(BF16))
        routed = jnp.sum(jnp.take(ybuf, pos, axis=0).reshape(t, TOP_K, d).astype(F32), axis=1)

        xf, xb = _shared(x1, x1b, routed, sh_w1[l].astype(BF16), sh_w3[l].astype(BF16),
                         sh_w2[l].astype(BF16), ln2_g[l].astype(F32)[None, :],
                         ln2_b[l].astype(F32)[None, :])
    return xf.reshape(batch, seq, d).astype(x.dtype)
```
